```python
import math
import jax, jax.numpy as jnp
from jax import lax
import numpy as np

D_MODEL = 1024
BATCH = 16
SEQ = 2048
DEPTH = 4

N_META = 16
D_MIX = D_MODEL
HEAD_DIM = 64
RWKV_WIDTH = D_MIX // 2
SB_WIDTH = D_MIX - RWKV_WIDTH
RWKV_HEADS = RWKV_WIDTH // HEAD_DIM
SB_HEADS = SB_WIDTH // HEAD_DIM
DECAY_RANK = 64
A_RANK = 64
GATE_RANK = 128
D_FF = 4 * D_MODEL
SB_BLOCK = 128
RMS_EPS = 1e-6
GN_EPS = 1e-5 * HEAD_DIM
RWKV_SPLITS = [RWKV_WIDTH, 2 * RWKV_WIDTH, 3 * RWKV_WIDTH,
               3 * RWKV_WIDTH + DECAY_RANK, 3 * RWKV_WIDTH + DECAY_RANK + A_RANK]
RWKV_COLS = 3 * RWKV_WIDTH + DECAY_RANK + A_RANK + GATE_RANK
SB_COLS = 3 * SB_WIDTH
IN_COLS = RWKV_COLS + SB_COLS

kernel_name = "hymba_rwkv7_stickbreak_hybrid"


def rmsnorm(x, gain):
    x32 = x.astype(jnp.float32)
    y = x32 * lax.rsqrt(jnp.mean(jnp.square(x32), axis=-1, keepdims=True) + RMS_EPS) * gain.astype(jnp.float32)
    return y.astype(x.dtype)


def rwkv7_group(p, mu, w_decay_up, decay_bias, w_a_up, a_bias, w_g_up, k_k, k_a, r_k, lnx_gain, lnx_bias):
    B, T, _ = p.shape
    H, N = RWKV_HEADS, HEAD_DIM
    p = p.astype(jnp.float32)
    prev = jnp.pad(p[:, :-1], ((0, 0), (1, 0), (0, 0)))
    p = p + (prev - p) * mu.astype(jnp.float32)
    r, k, v, dw, da, dg = jnp.split(p, RWKV_SPLITS, axis=-1)
    w_log = -jax.nn.softplus(-(decay_bias + jnp.tanh(dw) @ w_decay_up)) - 0.5
    decay = jnp.exp(-jnp.exp(w_log))
    a = jax.nn.sigmoid(a_bias + da @ w_a_up)
    g = jax.nn.sigmoid(dg) @ w_g_up
    heads = lambda t: t.reshape(B, T, H, N)
    kk = heads(k * k_k)
    kk = kk * lax.rsqrt(jnp.sum(jnp.square(kk), axis=-1, keepdims=True) + 1e-12)
    k = k * (1.0 + (a - 1.0) * k_a)
    rh, kh, vh, wh, ah = heads(r), heads(k), heads(v), heads(decay), heads(a)

    def step(S, inp):
        r_t, w_t, k_t, v_t, kk_t, a_t = inp
        sa = jnp.einsum('bhvk,bhk->bhv', S, -kk_t)
        S = (S * w_t[:, :, None, :] + sa[..., None] * (kk_t * a_t)[:, :, None, :]
             + v_t[..., None] * k_t[:, :, None, :])
        return S, jnp.einsum('bhvk,bhk->bhv', S, r_t)

    xs = tuple(jnp.moveaxis(t, 1, 0) for t in (rh, wh, kh, vh, kk, ah))
    S0 = jnp.zeros((B, H, N, N), jnp.float32)
    _, y = lax.scan(step, S0, xs)
    y = jnp.moveaxis(y, 0, 1)
    mean = jnp.mean(y, axis=-1, keepdims=True)
    var = jnp.mean(jnp.square(y - mean), axis=-1, keepdims=True)
    y = (y - mean) * lax.rsqrt(var + GN_EPS) * lnx_gain.reshape(H, N) + lnx_bias.reshape(H, N)
    y = y + jnp.sum(rh * kh * r_k, axis=-1, keepdims=True) * vh
    return y.reshape(B, T, H * N) * g


def stick_breaking_group(q, k, v, out_gain):
    B, T, _ = q.shape
    H, N = SB_HEADS, HEAD_DIM
    to_heads = lambda t: t.astype(jnp.float32).reshape(B, T, H, N).transpose(0, 2, 1, 3)
    qh, kh, vh = to_heads(q), to_heads(k), to_heads(v)
    scale = 1.0 / math.sqrt(N)
    bounds = [(0, N_META)] + [(s, min(s + SB_BLOCK, T)) for s in range(N_META, T, SB_BLOCK)]
    outs = []
    for q0, q1 in bounds:
        qb, kb, vb = qh[:, :, q0:q1], kh[:, :, :q1], vh[:, :, :q1]
        z = jnp.einsum('bhqd,bhkd->bhqk', qb, kb) * scale
        t_idx = q0 + jnp.arange(q1 - q0)
        s_idx = jnp.arange(q1)
        mask = s_idx[None, :] < t_idx[:, None]
        log_beta = jax.nn.log_sigmoid(z)
        log_1m = jnp.where(mask, log_beta - z, 0.0)
        log_a = log_beta + lax.cumsum(log_1m, axis=3, reverse=True) - log_1m
        A = jnp.where(mask, jnp.exp(log_a), 0.0)
        outs.append(jnp.einsum('bhqk,bhkd->bhqd', A, vb))
    o = jnp.concatenate(outs, axis=2)
    o = o * lax.rsqrt(jnp.mean(jnp.square(o), axis=-1, keepdims=True) + RMS_EPS)
    o = o * out_gain.astype(jnp.float32).reshape(H, N)[None, :, None, :]
    return o.transpose(0, 2, 1, 3).reshape(B, T, H * N)


def setup_inputs(seed: int = 0) -> dict:
    key = jax.random.key(seed)
    ks = jax.random.split(key, 24)
    f = jnp.float32
    nrm = lambda k, shape, fan_in: jax.random.normal(k, shape, f) * (fan_in ** -0.5)
    gain = lambda k, shape: 1.0 + 0.05 * jax.random.normal(k, shape, f)
    return {
        "x": jax.random.normal(ks[0], (BATCH, SEQ, D_MODEL), f),
        "meta_tokens": jax.random.normal(ks[1], (N_META, D_MODEL), f),
        "g_pre_mix": gain(ks[2], (DEPTH, D_MODEL)),
        "g_post_mix": gain(ks[3], (DEPTH, D_MODEL)),
        "g_pre_ffn": gain(ks[4], (DEPTH, D_MODEL)),
        "g_post_ffn": gain(ks[5], (DEPTH, D_MODEL)),
        "w_in": nrm(ks[6], (DEPTH, D_MODEL, IN_COLS), D_MODEL),
        "tshift_mu": jax.random.uniform(ks[7], (DEPTH, RWKV_COLS), f),
        "w_decay_up": 0.1 * nrm(ks[8], (DEPTH, DECAY_RANK, RWKV_WIDTH), DECAY_RANK),
        "decay_bias": jax.random.uniform(ks[9], (DEPTH, RWKV_WIDTH), f, -5.0, 1.0),
        "w_a_up": nrm(ks[10], (DEPTH, A_RANK, RWKV_WIDTH), A_RANK),
        "a_bias": 0.1 * jax.random.normal(ks[11], (DEPTH, RWKV_WIDTH), f),
        "w_g_up": nrm(ks[12], (DEPTH, GATE_RANK, RWKV_WIDTH), GATE_RANK),
        "k_k": 0.85 + 0.05 * jax.random.normal(ks[13], (DEPTH, RWKV_WIDTH), f),
        "k_a": gain(ks[14], (DEPTH, RWKV_WIDTH)),
        "r_k": 0.1 * jax.random.normal(ks[15], (DEPTH, RWKV_HEADS, HEAD_DIM), f),
        "lnx_gain": gain(ks[16], (DEPTH, RWKV_WIDTH)),
        "lnx_bias": 0.02 * jax.random.normal(ks[17], (DEPTH, RWKV_WIDTH), f),
        "sb_out_gain": gain(ks[18], (DEPTH, SB_WIDTH)),
        "w_out": nrm(ks[19], (DEPTH, D_MIX, D_MODEL), D_MIX),
        "w_ffn_up": nrm(ks[20], (DEPTH, D_MODEL, D_FF), D_MODEL),
        "w_ffn_down": nrm(ks[21], (DEPTH, D_FF, D_MODEL), D_FF),
    }


def reference(x, meta_tokens, g_pre_mix, g_post_mix, g_pre_ffn, g_post_ffn, w_in, tshift_mu,
              w_decay_up, decay_bias, w_a_up, a_bias, w_g_up, k_k, k_a, r_k, lnx_gain, lnx_bias,
              sb_out_gain, w_out, w_ffn_up, w_ffn_down):
    B = x.shape[0]
    meta = jnp.broadcast_to(meta_tokens[None].astype(x.dtype), (B, N_META, D_MODEL))
    h = jnp.concatenate([meta, x], axis=1)
    for l in range(DEPTH):
        xn = rmsnorm(h, g_pre_mix[l])
        p = xn @ w_in[l]
        p_rwkv, p_sb = p[..., :RWKV_COLS], p[..., RWKV_COLS:]
        y_a = rwkv7_group(p_rwkv, tshift_mu[l], w_decay_up[l], decay_bias[l], w_a_up[l], a_bias[l],
                          w_g_up[l], k_k[l], k_a[l], r_k[l], lnx_gain[l], lnx_bias[l])
        q, k, v = jnp.split(p_sb, 3, axis=-1)
        y_b = stick_breaking_group(q, k, v, sb_out_gain[l])
        y = jnp.concatenate([y_a, y_b], axis=-1).astype(h.dtype) @ w_out[l]
        h = h + rmsnorm(y, g_post_mix[l])
        m = rmsnorm(h, g_pre_ffn[l])
        m = jnp.square(jax.nn.relu(m @ w_ffn_up[l])) @ w_ffn_down[l]
        h = h + rmsnorm(m, g_post_ffn[l])
    return h[:, N_META:]
```

```python
import functools

import jax
import jax.numpy as jnp
from jax import lax
from jax.experimental import pallas as pl
from jax.experimental.pallas import tpu as pltpu

F32 = jnp.float32
BF16 = jnp.bfloat16

N_META = 16
HEAD_DIM = 64
N_HEADS = 8
GROUP_WIDTH = N_HEADS * HEAD_DIM
DECAY_RANK = 64
A_RANK = 64
GATE_RANK = 128
RWKV_COLS = 3 * GROUP_WIDTH + DECAY_RANK + A_RANK + GATE_RANK
SB_COLS = 3 * GROUP_WIDTH
RMS_EPS = 1e-6
GN_EPS = 1e-5 * HEAD_DIM
SB_BLOCK = 128

V7X_SUBLANES = 8
V7X_LANES = 128
V7X_VMEM_LIMIT_BYTES = 56 * 1024 * 1024


def _pick_tile(n, target, mult):
    best = None
    for t in range(mult, min(n, target) + 1, mult):
        if n % t == 0:
            best = t
    assert best is not None, (n, target, mult)
    return best


def _resident(shape):
    return pl.BlockSpec(shape, lambda *_: (0,) * len(shape), pipeline_mode=pl.Buffered(1))


def _rms(x):
    return x * lax.rsqrt(jnp.mean(x * x, axis=-1, keepdims=True) + RMS_EPS)


def _sigmoid(x):
    return 1.0 / (1.0 + jnp.exp(-x))


def _in_prep_kernel(h_ref, gpre_ref, win_ref, mu_ref, wd_ref, dbias_ref, wa_ref, abias_ref, wg_ref,
                    r_ref, k_ref, v_ref, w_ref, a_ref, g_ref, qs_ref, ks_ref, vs_ref, shift_ref, *, tm):
    gw = GROUP_WIDTH
    xn = (_rms(h_ref[0]) * gpre_ref[...]).astype(BF16)
    p = jnp.dot(xn, win_ref[...], preferred_element_type=F32)

    @pl.when(pl.program_id(1) == 0)
    def _():
        shift_ref[0:V7X_SUBLANES, :] = jnp.zeros((V7X_SUBLANES, RWKV_COLS), F32)

    pr = p[:, :RWKV_COLS]
    shift_ref[V7X_SUBLANES:V7X_SUBLANES + tm, :] = pr
    prev = shift_ref[V7X_SUBLANES - 1:V7X_SUBLANES - 1 + tm, :]
    shift_ref[V7X_SUBLANES - 1:V7X_SUBLANES, :] = shift_ref[V7X_SUBLANES - 1 + tm:V7X_SUBLANES + tm, :]
    m = pr + (prev - pr) * mu_ref[...]

    r_ref[0] = m[:, 0:gw]
    k_ref[0] = m[:, gw:2 * gw]
    v_ref[0] = m[:, 2 * gw:3 * gw]
    c0 = 3 * gw
    dw = m[:, c0:c0 + DECAY_RANK]
    da = m[:, c0 + DECAY_RANK:c0 + DECAY_RANK + A_RANK]
    dg = m[:, c0 + DECAY_RANK + A_RANK:RWKV_COLS]

    u = -(dbias_ref[...] + jnp.dot(jnp.tanh(dw).astype(BF16), wd_ref[...], preferred_element_type=F32))
    softplus = jnp.maximum(u, 0.0) + jnp.log(1.0 + jnp.exp(-jnp.abs(u)))
    w_ref[0] = jnp.exp(-jnp.exp(-softplus - 0.5))
    a_ref[0] = _sigmoid(abias_ref[...] + jnp.dot(da.astype(BF16), wa_ref[...], preferred_element_type=F32))
    g_ref[0] = jnp.dot(_sigmoid(dg).astype(BF16), wg_ref[...], preferred_element_type=F32)

    ps = p[:, RWKV_COLS:]
    q = (ps[:, 0:gw] * (HEAD_DIM ** -0.5)).astype(BF16)
    ks = ps[:, gw:2 * gw].astype(BF16)
    vs = ps[:, 2 * gw:3 * gw].astype(BF16)
    for hh in range(N_HEADS):
        sl = slice(hh * HEAD_DIM, (hh + 1) * HEAD_DIM)
        qs_ref[0, hh] = q[:, sl]
        ks_ref[0, hh] = ks[:, sl]
        vs_ref[0, hh] = vs[:, sl]


def _in_prep(h, gpre, win, mu, wd, dbias, wa, abias, wg):
    B, T, D = h.shape
    tm = _pick_tile(T, 352, V7X_SUBLANES)
    gw = GROUP_WIDTH
    tok = lambda width: pl.BlockSpec((1, tm, width), lambda b, i: (b, i, 0))
    heads = pl.BlockSpec((1, N_HEADS, tm, HEAD_DIM), lambda b, i: (b, 0, i, 0))
    f32_tok = jax.ShapeDtypeStruct((B, T, gw), F32)
    bf_heads = jax.ShapeDtypeStruct((B, N_HEADS, T, HEAD_DIM), BF16)
    return pl.pallas_call(
        functools.partial(_in_prep_kernel, tm=tm),
        grid=(B, T // tm),
        in_specs=[tok(D), _resident((1, D)), _resident(win.shape), _resident((1, RWKV_COLS)),
                  _resident(wd.shape), _resident((1, gw)), _resident(wa.shape), _resident((1, gw)),
                  _resident(wg.shape)],
        out_specs=[tok(gw)] * 6 + [heads] * 3,
        out_shape=[f32_tok] * 6 + [bf_heads] * 3,
        scratch_shapes=[pltpu.VMEM((tm + V7X_SUBLANES, RWKV_COLS), F32)],
        compiler_params=pltpu.CompilerParams(
            dimension_semantics=("arbitrary", "arbitrary"), vmem_limit_bytes=V7X_VMEM_LIMIT_BYTES),
        name="in_prep",
    )(h, gpre, win, mu, wd, dbias, wa, abias, wg)


def _scan_kernel(r_ref, k_ref, v_ref, w_ref, a_ref, kk_p, ka_p, rk_p, lg_p, lb_p, y_ref,
                 s_ref, kk_s, nb_s, kc_s, ys_s, *, tc):
    n = HEAD_DIM

    @pl.when(pl.program_id(0) == 0)
    def _():
        s_ref[...] = jnp.zeros(s_ref.shape, F32)

    k = k_ref[...]
    a = a_ref[...]
    kk = k * kk_p[...][None]
    kk = kk * lax.rsqrt(jnp.sum(kk * kk, axis=1, keepdims=True) + 1e-12)
    kc = k * (1.0 + (a - 1.0) * ka_p[...][None])
    kk_s[...] = kk
    nb_s[...] = -(kk * a)
    kc_s[...] = kc

    def step(t, carry):
        acc = [jnp.zeros((n, V7X_LANES), F32), jnp.zeros((n, V7X_LANES), F32)]
        for j in range(n):
            acc[j % 2] = acc[j % 2] + s_ref[j] * kk_s[t, j:j + 1, :]
        sa = acc[0] + acc[1]
        vv = v_ref[t]
        yacc = [jnp.zeros((n, V7X_LANES), F32), jnp.zeros((n, V7X_LANES), F32)]
        for j in range(n):
            s = (s_ref[j] * w_ref[t, j:j + 1, :] + sa * nb_s[t, j:j + 1, :]
                 + vv * kc_s[t, j:j + 1, :])
            s_ref[j] = s
            yacc[j % 2] = yacc[j % 2] + s * r_ref[t, j:j + 1, :]
        ys_s[t] = yacc[0] + yacc[1]
        return carry

    lax.fori_loop(0, tc, step, 0)

    y = ys_s[...]
    mean = jnp.mean(y, axis=1, keepdims=True)
    yc = y - mean
    var = jnp.mean(yc * yc, axis=1, keepdims=True)
    yn = yc * lax.rsqrt(var + GN_EPS) * lg_p[...][None] + lb_p[...][None]
    bonus = jnp.sum(r_ref[...] * kc * rk_p[...][None], axis=1, keepdims=True)
    y_ref[...] = yn + bonus * v_ref[...]


def _scan(r, k, v, w, a, kk_p, ka_p, rk_p, lg_p, lb_p):
    T, n, lanes = r.shape
    assert n == HEAD_DIM and lanes == V7X_LANES, r.shape
    tc = _pick_tile(T, 48, 1)
    seq = pl.BlockSpec((tc, n, lanes), lambda i: (i, 0, 0))
    par = _resident((n, lanes))
    return pl.pallas_call(
        functools.partial(_scan_kernel, tc=tc),
        grid=(T // tc,),
        in_specs=[seq] * 5 + [par] * 5,
        out_specs=seq,
        out_shape=jax.ShapeDtypeStruct((T, n, lanes), F32),
        scratch_shapes=[pltpu.VMEM((n, n, lanes), F32)] + [pltpu.VMEM((tc, n, lanes), F32)] * 4,
        compiler_params=pltpu.CompilerParams(
            dimension_semantics=("arbitrary",), vmem_limit_bytes=V7X_VMEM_LIMIT_BYTES),
        name="scan",
    )(r, k, v, w, a, kk_p, ka_p, rk_p, lg_p, lb_p)


def _sb_block(qb, kb, vb, cum_ones, c, mask):
    z = lax.dot_general(qb, kb, (((1,), (1,)), ((), ())), preferred_element_type=F32)
    soft = jnp.log(1.0 + jnp.exp(-jnp.abs(z)))
    log_beta = jnp.minimum(z, 0.0) - soft
    log_1m = jnp.minimum(-z, 0.0) - soft
    if mask is not None:
        log_1m = jnp.where(mask, log_1m, 0.0)
    hi = log_1m.astype(BF16)
    lo = (log_1m - hi.astype(F32)).astype(BF16)
    sums = (jnp.dot(hi, cum_ones, preferred_element_type=F32)
            + jnp.dot(lo, cum_ones, preferred_element_type=F32))
    w = jnp.exp(log_beta + sums[:, :SB_BLOCK] + c)
    if mask is not None:
        w = jnp.where(mask, w, 0.0)
    return jnp.dot(w.astype(BF16), vb, preferred_element_type=F32), c + sums[:, SB_BLOCK:]


def _sb_kernel(q_ref, k_ref, v_ref, gain_ref, o_ref, *, n_blocks):
    blk = SB_BLOCK
    row = lax.broadcasted_iota(jnp.int32, (blk, blk), 0)
    col = lax.broadcasted_iota(jnp.int32, (blk, blk), 1)
    causal = col < row
    meta_keys = col < N_META
    cum_ones = jnp.concatenate(
        [jnp.where(row > col, 1.0, 0.0), jnp.ones((blk, blk), F32)], axis=1).astype(BF16)
    gain = gain_ref[...]
    heads = range(q_ref.shape[1])

    def finish(outs):
        return (jnp.concatenate([_rms(o) for o in outs], axis=-1) * gain).astype(o_ref.dtype)

    outs = []
    for hh in heads:
        o, _ = _sb_block(q_ref[0, hh, 0:blk, :], k_ref[0, hh, 0:blk, :], v_ref[0, hh, 0:blk, :],
                         cum_ones, jnp.zeros((blk, blk), F32), causal)
        outs.append(o)
    o_ref[0, 0:N_META, :] = finish(outs)[0:N_META]

    def q_block(qi, carry):
        q0 = pl.multiple_of(N_META + qi * blk, N_META)
        qb = [q_ref[0, hh, pl.ds(q0, blk), :] for hh in heads]
        state = []
        for hh in heads:
            state.extend(_sb_block(qb[hh], k_ref[0, hh, pl.ds(q0, blk), :], v_ref[0, hh, pl.ds(q0, blk), :],
                                   cum_ones, jnp.zeros((blk, blk), F32), causal))

        def k_block(step, st):
            k0 = pl.multiple_of(N_META + (qi - 1 - step) * blk, N_META)
            new = []
            for hh in heads:
                o, c = st[2 * hh], st[2 * hh + 1]
                do, c = _sb_block(qb[hh], k_ref[0, hh, pl.ds(k0, blk), :], v_ref[0, hh, pl.ds(k0, blk), :],
                                  cum_ones, c, None)
                new.extend((o + do, c))
            return tuple(new)

        state = lax.fori_loop(0, qi, k_block, tuple(state))
        outs = []
        for hh in heads:
            do, _ = _sb_block(qb[hh], k_ref[0, hh, 0:blk, :], v_ref[0, hh, 0:blk, :],
                              cum_ones, state[2 * hh + 1], meta_keys)
            outs.append(state[2 * hh] + do)
        o_ref[0, pl.ds(q0, blk), :] = finish(outs)
        return carry

    lax.fori_loop(0, n_blocks, q_block, 0)


def _sb(qs, ks, vs, gain):
    B, H, T, n = qs.shape
    assert (T - N_META) % SB_BLOCK == 0 and n == HEAD_DIM, qs.shape
    pair = V7X_LANES // n
    heads = pl.BlockSpec((1, pair, T, n), lambda b, j: (b, j, 0, 0))
    return pl.pallas_call(
        functools.partial(_sb_kernel, n_blocks=(T - N_META) // SB_BLOCK),
        grid=(B, H // pair),
        in_specs=[heads] * 3 + [pl.BlockSpec((1, V7X_LANES), lambda b, j: (0, j))],
        out_specs=pl.BlockSpec((1, T, V7X_LANES), lambda b, j: (b, 0, j)),
        out_shape=jax.ShapeDtypeStruct((B, T, H * n), BF16),
        compiler_params=pltpu.CompilerParams(
            dimension_semantics=("arbitrary", "arbitrary"), vmem_limit_bytes=V7X_VMEM_LIMIT_BYTES),
        name="sb",
    )(qs, ks, vs, gain)


def _out_ffn_kernel(h_ref, ya_ref, g_ref, yb_ref, woa_ref, wob_ref, gpost_ref, gpre_ref, wup_ref,
                    wdown_ref, gpostf_ref, o_ref):
    ya = (ya_ref[...] * g_ref[...]).astype(BF16)
    y = (jnp.dot(ya, woa_ref[...], preferred_element_type=F32)
         + jnp.dot(yb_ref[...], wob_ref[...], preferred_element_type=F32))
    h = h_ref[...] + _rms(y) * gpost_ref[...]
    m = (_rms(h) * gpre_ref[...]).astype(BF16)
    u = jnp.maximum(jnp.dot(m, wup_ref[...], preferred_element_type=F32), 0.0)
    d = jnp.dot((u * u).astype(BF16), wdown_ref[...], preferred_element_type=F32)
    o_ref[...] = h + _rms(d) * gpostf_ref[...]


def _out_ffn(h, ya, g, yb, woa, wob, gpost, gpre, wup, wdown, gpostf):
    M, D = h.shape
    gw = ya.shape[1]
    tm = _pick_tile(M, 384, V7X_SUBLANES)
    tok = lambda width: pl.BlockSpec((tm, width), lambda i: (i, 0))
    return pl.pallas_call(
        _out_ffn_kernel,
        grid=(M // tm,),
        in_specs=[tok(D), tok(gw), tok(gw), tok(gw), _resident(woa.shape), _resident(wob.shape),
                  _resident((1, D)), _resident((1, D)), _resident(wup.shape), _resident(wdown.shape),
                  _resident((1, D))],
        out_specs=tok(D),
        out_shape=jax.ShapeDtypeStruct((M, D), F32),
        compiler_params=pltpu.CompilerParams(
            dimension_semantics=("arbitrary",), vmem_limit_bytes=V7X_VMEM_LIMIT_BYTES),
        name="out_ffn",
    )(h, ya, g, yb, woa, wob, gpost, gpre, wup, wdown, gpostf)


def kernel(x, meta_tokens, g_pre_mix, g_post_mix, g_pre_ffn, g_post_ffn, w_in, tshift_mu, w_decay_up,
           decay_bias, w_a_up, a_bias, w_g_up, k_k, k_a, r_k, lnx_gain, lnx_bias, sb_out_gain, w_out,
           w_ffn_up, w_ffn_down):
    B, _, D = x.shape
    depth = w_in.shape[0]
    gw = GROUP_WIDTH
    assert B * N_HEADS == V7X_LANES, "the scan puts the (batch, head) pairs on the lane axis"
    meta = jnp.broadcast_to(meta_tokens[None].astype(x.dtype), (B, N_META, D))
    h = jnp.concatenate([meta, x], axis=1)
    T = h.shape[1]

    row = lambda p: p.reshape(1, -1).astype(F32)
    pair_tile = lambda p: jnp.tile(p.reshape(N_HEADS, HEAD_DIM).T.astype(F32), (1, B))
    to_scan = lambda t: t.reshape(B, T, N_HEADS, HEAD_DIM).transpose(1, 3, 0, 2).reshape(T, HEAD_DIM, B * N_HEADS)

    for l in range(depth):
        r, k, v, w, a, g, qs, ks, vs = _in_prep(
            h, row(g_pre_mix[l]), w_in[l].astype(BF16), row(tshift_mu[l]), w_decay_up[l].astype(BF16),
            row(decay_bias[l]), w_a_up[l].astype(BF16), row(a_bias[l]), w_g_up[l].astype(BF16))
        y = _scan(to_scan(r), to_scan(k), to_scan(v), to_scan(w), to_scan(a),
                  pair_tile(k_k[l]), pair_tile(k_a[l]), pair_tile(r_k[l]), pair_tile(lnx_gain[l]),
                  pair_tile(lnx_bias[l]))
        ya = y.reshape(T, HEAD_DIM, B, N_HEADS).transpose(2, 0, 3, 1).reshape(B * T, gw)
        yb = _sb(qs, ks, vs, row(sb_out_gain[l]))
        wo = w_out[l].astype(BF16)
        h = _out_ffn(h.reshape(B * T, D), ya, g.reshape(B * T, gw), yb.reshape(B * T, gw), wo[:gw], wo[gw:],
                     row(g_post_mix[l]), row(g_pre_ffn[l]), w_ffn_up[l].astype(BF16),
                     w_ffn_down[l].astype(BF16), row(g_post_ffn[l])).reshape(B, T, D)
    return h[:, N_META:]
```
